```python
import jax, jax.numpy as jnp
from jax import lax
import numpy as np

D_MODEL = 1024
BATCH = 8
SEQ = 2048
DEPTH = 1

HEAD_DIM = 64
N_Q_HEADS = 8
N_KV_HEADS = 2
Q_PER_KV = N_Q_HEADS // N_KV_HEADS
ATTN_WIDTH = N_Q_HEADS * HEAD_DIM
KV_WIDTH = N_KV_HEADS * HEAD_DIM
WINDOW = 128
BLOCK = 128
LRU_WIDTH = D_MODEL - ATTN_WIDTH
LRU_HEADS = 8
LRU_HEAD_DIM = LRU_WIDTH // LRU_HEADS
LRU_CONV_WIDTH = 4
LRU_C = 8.0
MIX_WIDTH = ATTN_WIDTH + LRU_WIDTH
IN_WIDTH = ATTN_WIDTH + 2 * KV_WIDTH + 2 * LRU_WIDTH
D_FF = 2816
FFN_CONV_WIDTH = 3
RMS_EPS = 1e-6

kernel_name = "hymba_swa_sink_rglru_convffn_sandwich"


def rmsnorm(x, g):
    xf = x.astype(jnp.float32)
    y = xf * lax.rsqrt(jnp.mean(xf * xf, axis=-1, keepdims=True) + RMS_EPS)
    return (y * g.astype(jnp.float32)).astype(x.dtype)


def causal_dwconv(x, w, b):
    K = w.shape[0]
    S = x.shape[1]
    xp = jnp.pad(x, ((0, 0), (K - 1, 0), (0, 0)))
    y = xp[:, 0:S] * w[0]
    for k in range(1, K):
        y = y + xp[:, k:k + S] * w[k]
    return y + b


def alibi_slopes(n_heads):
    h = jnp.arange(1, n_heads + 1, dtype=jnp.float32)
    return jnp.exp2(-8.0 * h / n_heads)


def sliding_window_attention(q, k, v, sinks):
    B, S, _ = q.shape
    nb = S // BLOCK
    qb = q.reshape(B, nb, BLOCK, N_KV_HEADS, Q_PER_KV, HEAD_DIM)
    kb = k.reshape(B, nb, BLOCK, N_KV_HEADS, HEAD_DIM)
    vb = v.reshape(B, nb, BLOCK, N_KV_HEADS, HEAD_DIM)

    def with_prev(t):
        prev = jnp.pad(t[:, :-1], ((0, 0), (1, 0), (0, 0), (0, 0), (0, 0)))
        return jnp.concatenate([prev, t], axis=2)

    kk = with_prev(kb)
    vv = with_prev(vb)
    scale = HEAD_DIM ** -0.5
    scores = jnp.einsum('bnqhgd,bnkhd->bnhgqk', qb, kk).astype(jnp.float32) * scale

    qi = jnp.arange(BLOCK)[:, None]
    ki = jnp.arange(2 * BLOCK)[None, :]
    dist = (BLOCK + qi - ki)
    in_window = (dist >= 0) & (dist < WINDOW)
    key_exists = (jnp.arange(nb)[:, None, None] > 0) | (ki >= BLOCK)[None]
    mask = in_window[None] & key_exists

    slopes = alibi_slopes(N_Q_HEADS).reshape(N_KV_HEADS, Q_PER_KV)
    bias = -slopes[:, :, None, None] * dist.astype(jnp.float32)
    logits = scores + bias[None, None]
    logits = jnp.where(mask[None, :, None, None], logits, jnp.finfo(jnp.float32).min)

    s = sinks.astype(jnp.float32).reshape(N_KV_HEADS, Q_PER_KV)
    sink_col = jnp.broadcast_to(s[None, None, :, :, None, None], logits.shape[:-1] + (1,))
    probs = jax.nn.softmax(jnp.concatenate([logits, sink_col], axis=-1), axis=-1)[..., :-1]
    out = jnp.einsum('bnhgqk,bnkhd->bnqhgd', probs.astype(v.dtype), vv)
    return out.reshape(B, S, ATTN_WIDTH)


def rg_lru(x, w_a, b_a, w_x, b_x, lam):
    B, S, _ = x.shape
    xh = x.reshape(B, S, LRU_HEADS, LRU_HEAD_DIM)
    r = jax.nn.sigmoid(jnp.einsum('bshi,hij->bshj', xh, w_a).reshape(B, S, LRU_WIDTH) + b_a)
    i = jax.nn.sigmoid(jnp.einsum('bshi,hij->bshj', xh, w_x).reshape(B, S, LRU_WIDTH) + b_x)
    log_a = -LRU_C * r.astype(jnp.float32) * jax.nn.softplus(-lam.astype(jnp.float32))
    a = jnp.exp(log_a)
    mult = jnp.sqrt(-jnp.expm1(2.0 * log_a))
    u = mult * (i * x).astype(jnp.float32)

    def combine(left, right):
        a_l, b_l = left
        a_r, b_r = right
        return a_l * a_r, a_r * b_l + b_r

    _, h = lax.associative_scan(combine, (a, u), axis=1)
    return h.astype(x.dtype)


def setup_inputs(seed: int = 0) -> dict:
    key = jax.random.key(seed)
    ks = jax.random.split(key, 24)
    f32 = jnp.float32

    def nrm(k, shape, scale):
        return jax.random.normal(k, shape, f32) * scale

    def gain(k, n):
        return 1.0 + 0.05 * jax.random.normal(k, (DEPTH, n), f32)

    x = jax.random.normal(ks[0], (BATCH, SEQ, D_MODEL), f32)
    a0 = jax.random.uniform(ks[10], (DEPTH, LRU_WIDTH), f32, 0.9, 0.999)
    base = a0 ** (1.0 / LRU_C)
    lru_lambda = jnp.log(base) - jnp.log1p(-base)
    return {
        "x": x,
        "norm_mix_pre": gain(ks[1], D_MODEL),
        "w_in": nrm(ks[2], (DEPTH, D_MODEL, IN_WIDTH), D_MODEL ** -0.5),
        "sinks": nrm(ks[3], (DEPTH, N_Q_HEADS), 0.5),
        "lru_conv_w": nrm(ks[4], (DEPTH, LRU_CONV_WIDTH, LRU_WIDTH), LRU_CONV_WIDTH ** -0.5),
        "lru_conv_b": nrm(ks[5], (DEPTH, LRU_WIDTH), 0.01),
        "lru_wa": nrm(ks[6], (DEPTH, LRU_HEADS, LRU_HEAD_DIM, LRU_HEAD_DIM), LRU_HEAD_DIM ** -0.5),
        "lru_ba": nrm(ks[7], (DEPTH, LRU_WIDTH), 0.01),
        "lru_wx": nrm(ks[8], (DEPTH, LRU_HEADS, LRU_HEAD_DIM, LRU_HEAD_DIM), LRU_HEAD_DIM ** -0.5),
        "lru_bx": nrm(ks[9], (DEPTH, LRU_WIDTH), 0.01),
        "lru_lambda": lru_lambda,
        "norm_attn_out": gain(ks[11], ATTN_WIDTH),
        "norm_lru_out": gain(ks[12], LRU_WIDTH),
        "w_out": nrm(ks[13], (DEPTH, MIX_WIDTH, D_MODEL), MIX_WIDTH ** -0.5),
        "norm_mix_post": gain(ks[14], D_MODEL),
        "norm_ffn_pre": gain(ks[15], D_MODEL),
        "w_up": nrm(ks[16], (DEPTH, D_MODEL, 2 * D_FF), D_MODEL ** -0.5),
        "ffn_conv_w": nrm(ks[17], (DEPTH, FFN_CONV_WIDTH, 2 * D_FF), FFN_CONV_WIDTH ** -0.5),
        "ffn_conv_b": nrm(ks[18], (DEPTH, 2 * D_FF), 0.01),
        "w_down": nrm(ks[19], (DEPTH, D_FF, D_MODEL), D_FF ** -0.5),
        "norm_ffn_post": gain(ks[20], D_MODEL),
    }


def reference(x, norm_mix_pre, w_in, sinks, lru_conv_w, lru_conv_b, lru_wa, lru_ba,
              lru_wx, lru_bx, lru_lambda, norm_attn_out, norm_lru_out, w_out,
              norm_mix_post, norm_ffn_pre, w_up, ffn_conv_w, ffn_conv_b, w_down,
              norm_ffn_post):
    splits = [ATTN_WIDTH, ATTN_WIDTH + KV_WIDTH, ATTN_WIDTH + 2 * KV_WIDTH,
              ATTN_WIDTH + 2 * KV_WIDTH + LRU_WIDTH]
    for l in range(DEPTH):
        h = rmsnorm(x, norm_mix_pre[l])
        proj = h @ w_in[l]
        q, k, v, lx, lg = jnp.split(proj, splits, axis=-1)
        attn = sliding_window_attention(q, k, v, sinks[l])
        lx = causal_dwconv(lx, lru_conv_w[l], lru_conv_b[l])
        lru = rg_lru(lx, lru_wa[l], lru_ba[l], lru_wx[l], lru_bx[l], lru_lambda[l])
        lru = lru * jax.nn.gelu(lg, approximate=True)
        merged = jnp.concatenate([rmsnorm(attn, norm_attn_out[l]),
                                  rmsnorm(lru, norm_lru_out[l])], axis=-1)
        x = x + rmsnorm(merged @ w_out[l], norm_mix_post[l])
        f = rmsnorm(x, norm_ffn_pre[l]) @ w_up[l]
        f = causal_dwconv(f, ffn_conv_w[l], ffn_conv_b[l])
        gate, val = jnp.split(f, 2, axis=-1)
        f = (jax.nn.gelu(gate, approximate=True) * val) @ w_down[l]
        x = x + rmsnorm(f, norm_ffn_post[l])
    return x
```

```python
import functools
import math

import jax
import jax.numpy as jnp
from jax import lax
from jax.experimental import pallas as pl
from jax.experimental.pallas import tpu as pltpu

D_MODEL = 1024
HEAD_DIM = 64
N_Q_HEADS = 8
N_KV_HEADS = 2
ATTN_WIDTH = N_Q_HEADS * HEAD_DIM
KV_WIDTH = N_KV_HEADS * HEAD_DIM
BLOCK = 128
LRU_WIDTH = D_MODEL - ATTN_WIDTH
LRU_HEADS = 8
LRU_HEAD_DIM = LRU_WIDTH // LRU_HEADS
LRU_CONV_WIDTH = 4
LRU_C = 8.0
IN_WIDTH = ATTN_WIDTH + 2 * KV_WIDTH + 2 * LRU_WIDTH
D_FF = 2816
FFN_CONV_WIDTH = 3
RMS_EPS = 1e-6

LANES = 128
SUBLANES = 8
HALO = SUBLANES
MIX_TILE = 512
FFN_TILE = 512
FFN_CHUNK = 256
VMEM_LIMIT = 56 * 1024 * 1024

_F32_MIN = float(jnp.finfo(jnp.float32).min)
_SQRT_2_OVER_PI = math.sqrt(2.0 / math.pi)


def _rmsnorm(x, g):
    ms = jnp.mean(x * x, axis=-1, keepdims=True)
    return x * lax.rsqrt(ms + RMS_EPS) * g


def _gelu_tanh(x):
    return x * (0.5 * (1.0 + jnp.tanh(_SQRT_2_OVER_PI * (x + 0.044715 * (x * x * x)))))


def _sigmoid(x):
    return 1.0 / (1.0 + jnp.exp(-x))


def _causal_conv(xe, w_ref, b_ref, width, rows):
    y = xe[HALO:HALO + rows] * w_ref[width - 1:width, :]
    for k in range(width - 1):
        shifted = pltpu.roll(xe, width - 1 - k, axis=0)[HALO:HALO + rows]
        y = y + shifted * w_ref[k:k + 1, :]
    return y + b_ref[...]


def _scan_shift(x, d, fill, row_ids):
    rows = x.shape[0]
    if d % SUBLANES == 0:
        pad = jnp.full((d, x.shape[1]), fill, x.dtype)
        return jnp.concatenate([pad, x[:rows - d]], axis=0)
    return jnp.where(row_ids >= d, pltpu.roll(x, d, axis=0), fill)


def _linear_scan(a, b):
    rows = a.shape[0]
    row_ids = lax.broadcasted_iota(jnp.int32, a.shape, 0)
    d = 1
    while d < rows:
        b = b + a * _scan_shift(b, d, 0.0, row_ids)
        a = a * _scan_shift(a, d, 1.0, row_ids)
        d *= 2
    return a, b


def _mixer_kernel(sinks_ref, x_ref, gpre_ref, win_ref, cw_ref, cb_ref, wg_ref, ba_ref, bx_ref,
                  lam_ref, gattn_ref, glru_ref, wout_ref, gpost_ref, out_ref,
                  proj_ref, kv_ref, attn_ref, halo_ref, hstate_ref):
    T = MIX_TILE
    s = pl.program_id(1)

    @pl.when(s == 0)
    def _():
        kv_ref[:, 0:BLOCK, :] = jnp.zeros((8, BLOCK, LANES), jnp.bfloat16)
        halo_ref[...] = jnp.zeros_like(halo_ref)
        hstate_ref[...] = jnp.zeros_like(hstate_ref)

    x = x_ref[0]
    h = _rmsnorm(x, gpre_ref[...]).astype(jnp.bfloat16)
    proj_ref[...] = jnp.dot(h, win_ref[...], preferred_element_type=jnp.float32)

    kvf = proj_ref[:, ATTN_WIDTH:ATTN_WIDTH + 2 * KV_WIDTH]
    lane = lax.broadcasted_iota(jnp.int32, (T, LANES), 1)
    low = lane < HEAD_DIM
    for which in range(2):
        t = kvf[:, which * KV_WIDTH:(which + 1) * KV_WIDTH]
        lo = jnp.where(low, t, 0.0)
        hi = jnp.where(low, 0.0, t)
        lo_r = pltpu.roll(lo, HEAD_DIM, axis=1)
        hi_r = pltpu.roll(hi, HEAD_DIM, axis=1)
        for idx, val in ((0, lo), (1, lo_r), (2, hi_r), (3, hi)):
            kv_ref[which * 4 + idx, BLOCK:BLOCK + T, :] = val.astype(jnp.bfloat16)

    qi = lax.broadcasted_iota(jnp.int32, (BLOCK, 2 * BLOCK), 0)
    ci = lax.broadcasted_iota(jnp.int32, (BLOCK, 2 * BLOCK), 1)
    dist_i = BLOCK + qi - ci
    in_window = (dist_i >= 0) & (dist_i < BLOCK)
    dist = dist_i.astype(jnp.float32)
    has_prev = jnp.where(s > 0, 1, 0)
    lane_b = lax.broadcasted_iota(jnp.int32, (BLOCK, LANES), 1)
    low_b = lane_b < HEAD_DIM

    for j in range(T // BLOCK):
        r0 = j * BLOCK
        if j == 0:
            mask = in_window & ((ci >= BLOCK) | (has_prev > 0))
        else:
            mask = in_window
        qs = (proj_ref[r0:r0 + BLOCK, 0:ATTN_WIDTH] * (HEAD_DIM ** -0.5)).astype(jnp.bfloat16)
        probs = [None] * N_Q_HEADS
        inv_den = [None] * N_Q_HEADS
        for g in range(N_KV_HEADS):
            rows = jnp.concatenate([qs[:, (2 * g) * LANES:(2 * g + 1) * LANES],
                                    qs[:, (2 * g + 1) * LANES:(2 * g + 2) * LANES]], axis=0)
            for par in range(2):
                keys = kv_ref[g * 2 + par, r0:r0 + 2 * BLOCK, :]
                sc = lax.dot_general(rows, keys, (((1,), (1,)), ((), ())),
                                     preferred_element_type=jnp.float32)
                for half in range(2):
                    head = 4 * g + 2 * half + par
                    slope = 2.0 ** (-8.0 * (head + 1) / N_Q_HEADS)
                    logit = sc[half * BLOCK:(half + 1) * BLOCK] - slope * dist
                    logit = jnp.where(mask, logit, _F32_MIN)
                    sink = sinks_ref[0, head]
                    m = jnp.maximum(jnp.max(logit, axis=-1, keepdims=True), sink)
                    p = jnp.exp(logit - m)
                    den = jnp.sum(p, axis=-1, keepdims=True) + jnp.exp(sink - m)
                    probs[head] = p.astype(jnp.bfloat16)
                    inv_den[head] = 1.0 / den
        for slab in range(ATTN_WIDTH // LANES):
            g = slab // 2
            h_lo, h_hi = 2 * slab, 2 * slab + 1
            v_lo = kv_ref[4 + g * 2 + 0, r0:r0 + 2 * BLOCK, :]
            v_hi = kv_ref[4 + g * 2 + 1, r0:r0 + 2 * BLOCK, :]
            o = (jnp.dot(probs[h_lo], v_lo, preferred_element_type=jnp.float32)
                 + jnp.dot(probs[h_hi], v_hi, preferred_element_type=jnp.float32))
            o = o * jnp.where(low_b, inv_den[h_lo], inv_den[h_hi])
            attn_ref[r0:r0 + BLOCK, slab * LANES:(slab + 1) * LANES] = o

    kv_ref[:, 0:BLOCK, :] = kv_ref[:, T:T + BLOCK, :]

    lx0 = ATTN_WIDTH + 2 * KV_WIDTH
    lx = proj_ref[:, lx0:lx0 + LRU_WIDTH]
    xe = jnp.concatenate([halo_ref[...], lx], axis=0)
    halo_ref[...] = lx[T - HALO:T]
    lxc = _causal_conv(xe, cw_ref, cb_ref, LRU_CONV_WIDTH, T)
    lxc_b = lxc.astype(jnp.bfloat16)
    half_w = LRU_WIDTH // 2
    gates = [jnp.dot(lxc_b[:, hf * half_w:(hf + 1) * half_w], wg_ref[hf],
                     preferred_element_type=jnp.float32) for hf in range(2)]
    r_pre = jnp.concatenate([gates[0][:, :half_w], gates[1][:, :half_w]], axis=1) + ba_ref[...]
    i_pre = jnp.concatenate([gates[0][:, half_w:], gates[1][:, half_w:]], axis=1) + bx_ref[...]
    z = -lam_ref[...]
    softplus = jnp.maximum(z, 0.0) + jnp.log1p(jnp.exp(-jnp.abs(z)))
    log_a = (-LRU_C) * _sigmoid(r_pre) * softplus
    a = jnp.exp(log_a)
    mult = jnp.sqrt(-jnp.tanh(log_a) * (a * a + 1.0))
    u = mult * (_sigmoid(i_pre) * lxc)
    a_cum, hs = _linear_scan(a, u)
    hs = hs + a_cum * hstate_ref[0:1, :]
    hstate_ref[...] = jnp.broadcast_to(hs[T - 1:T, :], hstate_ref.shape)
    lg = proj_ref[:, lx0 + LRU_WIDTH:lx0 + 2 * LRU_WIDTH]
    lru = hs * _gelu_tanh(lg)

    merged = jnp.concatenate([_rmsnorm(attn_ref[...], gattn_ref[...]).astype(jnp.bfloat16),
                              _rmsnorm(lru, glru_ref[...]).astype(jnp.bfloat16)], axis=1)
    y = jnp.dot(merged, wout_ref[...], preferred_element_type=jnp.float32)
    out_ref[0] = x + _rmsnorm(y, gpost_ref[...])


def _ffn_kernel(x_ref, gpre_ref, wup_ref, cw_ref, cb_ref, wdown_ref, gpost_ref, out_ref,
                h_ref, acc_ref, halo_ref):
    T = FFN_TILE
    s = pl.program_id(1)

    @pl.when(s == 0)
    def _():
        halo_ref[...] = jnp.zeros_like(halo_ref)

    x = x_ref[0]
    h_ref[...] = _rmsnorm(x, gpre_ref[...]).astype(jnp.bfloat16)
    for c in range(D_FF // FFN_CHUNK):
        acts = []
        for part in range(2):
            c0 = part * D_FF + c * FFN_CHUNK
            f = jnp.dot(h_ref[...], wup_ref[:, c0:c0 + FFN_CHUNK], preferred_element_type=jnp.float32)
            fe = jnp.concatenate([halo_ref[:, c0:c0 + FFN_CHUNK], f], axis=0)
            halo_ref[:, c0:c0 + FFN_CHUNK] = f[T - HALO:T]
            acts.append(_causal_conv(fe, cw_ref.at[:, c0:c0 + FFN_CHUNK], cb_ref.at[:, c0:c0 + FFN_CHUNK],
                                     FFN_CONV_WIDTH, T))
        act = (_gelu_tanh(acts[0]) * acts[1]).astype(jnp.bfloat16)
        contrib = jnp.dot(act, wdown_ref[c * FFN_CHUNK:(c + 1) * FFN_CHUNK, :],
                          preferred_element_type=jnp.float32)
        if c == 0:
            acc_ref[...] = contrib
        else:
            acc_ref[...] += contrib
    out_ref[0] = x + _rmsnorm(acc_ref[...], gpost_ref[...])


def _const_spec(shape):
    zeros = (0,) * len(shape)
    return pl.BlockSpec(shape, lambda b, s: zeros)


def _block_diag_gates(wa, wx):
    per_half = LRU_HEADS // 2
    halves = []
    for hf in range(2):
        mats = []
        for w in (wa, wx):
            blocks = [w[hf * per_half + i] for i in range(per_half)]
            mats.append(jax.scipy.linalg.block_diag(*blocks))
        halves.append(jnp.concatenate(mats, axis=1))
    return jnp.stack(halves).astype(jnp.bfloat16)


def kernel(x, norm_mix_pre, w_in, sinks, lru_conv_w, lru_conv_b, lru_wa, lru_ba, lru_wx, lru_bx,
           lru_lambda, norm_attn_out, norm_lru_out, w_out, norm_mix_post, norm_ffn_pre, w_up,
           ffn_conv_w, ffn_conv_b, w_down, norm_ffn_post):
    B, S, D = x.shape
    depth = w_in.shape[0]
    bf16 = jnp.bfloat16
    for l in range(depth):
        tile_spec = pl.BlockSpec((1, MIX_TILE, D), lambda b, s: (b, s, 0))
        x = pl.pallas_call(
            _mixer_kernel,
            grid=(B, S // MIX_TILE),
            in_specs=[
                pl.BlockSpec(memory_space=pltpu.SMEM),
                tile_spec,
                _const_spec((1, D)),
                _const_spec((D, IN_WIDTH)),
                _const_spec((LRU_CONV_WIDTH, LRU_WIDTH)),
                _const_spec((1, LRU_WIDTH)),
                _const_spec((2, LRU_WIDTH // 2, LRU_WIDTH)),
                _const_spec((1, LRU_WIDTH)),
                _const_spec((1, LRU_WIDTH)),
                _const_spec((1, LRU_WIDTH)),
                _const_spec((1, ATTN_WIDTH)),
                _const_spec((1, LRU_WIDTH)),
                _const_spec((D, D)),
                _const_spec((1, D)),
            ],
            out_specs=tile_spec,
            out_shape=jax.ShapeDtypeStruct((B, S, D), jnp.float32),
            scratch_shapes=[
                pltpu.VMEM((MIX_TILE, IN_WIDTH), jnp.float32),
                pltpu.VMEM((8, BLOCK + MIX_TILE, LANES), bf16),
                pltpu.VMEM((MIX_TILE, ATTN_WIDTH), jnp.float32),
                pltpu.VMEM((HALO, LRU_WIDTH), jnp.float32),
                pltpu.VMEM((SUBLANES, LRU_WIDTH), jnp.float32),
            ],
            compiler_params=pltpu.CompilerParams(
                dimension_semantics=("arbitrary", "arbitrary"), vmem_limit_bytes=VMEM_LIMIT),
            name="mixer",
        )(sinks[l].reshape(1, N_Q_HEADS), x, norm_mix_pre[l].reshape(1, D), w_in[l].astype(bf16),
          lru_conv_w[l], lru_conv_b[l].reshape(1, LRU_WIDTH), _block_diag_gates(lru_wa[l], lru_wx[l]),
          lru_ba[l].reshape(1, LRU_WIDTH), lru_bx[l].reshape(1, LRU_WIDTH),
          lru_lambda[l].reshape(1, LRU_WIDTH), norm_attn_out[l].reshape(1, ATTN_WIDTH),
          norm_lru_out[l].reshape(1, LRU_WIDTH), w_out[l].astype(bf16), norm_mix_post[l].reshape(1, D))

        ftile_spec = pl.BlockSpec((1, FFN_TILE, D), lambda b, s: (b, s, 0))
        x = pl.pallas_call(
            _ffn_kernel,
            grid=(B, S // FFN_TILE),
            in_specs=[
                ftile_spec,
                _const_spec((1, D)),
                _const_spec((D, 2 * D_FF)),
                _const_spec((FFN_CONV_WIDTH, 2 * D_FF)),
                _const_spec((1, 2 * D_FF)),
                _const_spec((D_FF, D)),
                _const_spec((1, D)),
            ],
            out_specs=ftile_spec,
            out_shape=jax.ShapeDtypeStruct((B, S, D), jnp.float32),
            scratch_shapes=[
                pltpu.VMEM((FFN_TILE, D), bf16),
                pltpu.VMEM((FFN_TILE, D), jnp.float32),
                pltpu.VMEM((HALO, 2 * D_FF), jnp.float32),
            ],
            compiler_params=pltpu.CompilerParams(
                dimension_semantics=("arbitrary", "arbitrary"), vmem_limit_bytes=VMEM_LIMIT),
            name="ffn",
        )(x, norm_ffn_pre[l].reshape(1, D), w_up[l].astype(bf16), ffn_conv_w[l],
          ffn_conv_b[l].reshape(1, 2 * D_FF), w_down[l].astype(bf16), norm_ffn_post[l].reshape(1, D))
    return x
```
